```python
import jax, jax.numpy as jnp
from jax import lax
import numpy as np

D_MODEL = 1024
BATCH = 4
SEQ = 8192
DEPTH = 2

CHUNK = 64
N_META = 16
MIX = D_MODEL
CONV_WIDTH_CH = MIX // 2
CONV_HEADS = 8
CONV_HEAD_DIM = CONV_WIDTH_CH // CONV_HEADS
CONV_K = 31
POOL_WIDTH_CH = MIX - CONV_WIDTH_CH
POOL_WINDOWS = (2, 4, 8, 16)
POOL_GROUPS = len(POOL_WINDOWS)
POOL_GROUP_DIM = POOL_WIDTH_CH // POOL_GROUPS
IN_COLS = 2 * CONV_WIDTH_CH + POOL_WIDTH_CH
D_FF = 2816
FFN_CONV_K = 3
EPS = 1e-6

kernel_name = "hybrid_conformer_conv_pool_encoder"


def rmsnorm(x, g):
    xf = x.astype(jnp.float32)
    y = xf * lax.rsqrt(jnp.mean(xf * xf, axis=-1, keepdims=True) + EPS)
    return (y * g.astype(jnp.float32)).astype(x.dtype)


def causal_dwconv(x, k):
    w, c = k.shape
    xp = jnp.pad(x, ((0, 0), (w - 1, 0), (0, 0)))
    return lax.conv_general_dilated(
        xp, k[:, None, :].astype(x.dtype), window_strides=(1,), padding="VALID",
        dimension_numbers=("NWC", "WIO", "NWC"), feature_group_count=c)


def conformer_conv_group(a, g, dw_k, dw_b, ln_g, ln_b):
    u = a * jax.nn.sigmoid(g)
    u = causal_dwconv(u, dw_k) + dw_b.astype(a.dtype)
    bsz, length, c = u.shape
    uh = u.reshape(bsz, length, CONV_HEADS, CONV_HEAD_DIM).astype(jnp.float32)
    mu = jnp.mean(uh, axis=-1, keepdims=True)
    var = jnp.mean(jnp.square(uh - mu), axis=-1, keepdims=True)
    uh = (uh - mu) * lax.rsqrt(var + EPS)
    u = uh.reshape(bsz, length, c) * ln_g.astype(jnp.float32) + ln_b.astype(jnp.float32)
    return jax.nn.silu(u).astype(a.dtype)


def multiscale_pool_group(p, pool_w, pool_scale):
    bsz, length, c = p.shape
    pf = p.astype(jnp.float32)
    cs = jnp.pad(jnp.cumsum(pf, axis=1), ((0, 0), (1, 0), (0, 0)))
    t = jnp.arange(length)
    outs = []
    for gi, w in enumerate(POOL_WINDOWS):
        sl = slice(gi * POOL_GROUP_DIM, (gi + 1) * POOL_GROUP_DIM)
        cg = cs[:, :, sl]
        upper = cg[:, 1:]
        lower = jnp.pad(cg[:, :length + 1 - w], ((0, 0), (w - 1, 0), (0, 0)))
        cnt = jnp.minimum(t + 1, w).astype(jnp.float32)[None, :, None]
        outs.append((upper - lower) / cnt - pf[:, :, sl])
    d = jnp.stack(outs, axis=2).astype(p.dtype)
    y = jnp.einsum("blgc,gcd->blgd", d, pool_w).reshape(bsz, length, c)
    return y * pool_scale


def setup_inputs(seed: int = 0) -> dict:
    key = jax.random.key(seed)
    ks = jax.random.split(key, 16)
    f32 = jnp.float32
    nrm = lambda k, shape, s: (jax.random.normal(k, shape, f32) * s)
    return {
        "x": nrm(ks[0], (BATCH, SEQ, D_MODEL), 1.0),
        "meta_tokens": nrm(ks[1], (N_META, D_MODEL), 1.0),
        "norm1_g": 1.0 + nrm(ks[2], (DEPTH, D_MODEL), 0.02),
        "w_in": nrm(ks[3], (DEPTH, D_MODEL, IN_COLS), D_MODEL ** -0.5),
        "conv_dw_k": nrm(ks[4], (DEPTH, CONV_K, CONV_WIDTH_CH), CONV_K ** -0.5),
        "conv_dw_b": nrm(ks[5], (DEPTH, CONV_WIDTH_CH), 0.02),
        "conv_ln_g": 1.0 + nrm(ks[6], (DEPTH, CONV_WIDTH_CH), 0.02),
        "conv_ln_b": nrm(ks[7], (DEPTH, CONV_WIDTH_CH), 0.02),
        "pool_w": nrm(ks[8], (DEPTH, POOL_GROUPS, POOL_GROUP_DIM, POOL_GROUP_DIM), POOL_GROUP_DIM ** -0.5),
        "pool_scale": 1.0 + nrm(ks[9], (DEPTH, POOL_WIDTH_CH), 0.02),
        "w_out": nrm(ks[10], (DEPTH, MIX, D_MODEL), MIX ** -0.5),
        "norm2_g": 1.0 + nrm(ks[11], (DEPTH, D_MODEL), 0.02),
        "w_up": nrm(ks[12], (DEPTH, D_MODEL, 2 * D_FF), D_MODEL ** -0.5),
        "ffn_dw_k": nrm(ks[13], (DEPTH, FFN_CONV_K, 2 * D_FF), FFN_CONV_K ** -0.5),
        "w_down": nrm(ks[14], (DEPTH, D_FF, D_MODEL), D_FF ** -0.5),
        "final_g": 1.0 + nrm(ks[15], (D_MODEL,), 0.02),
    }


def reference(x, meta_tokens, norm1_g, w_in, conv_dw_k, conv_dw_b, conv_ln_g, conv_ln_b,
              pool_w, pool_scale, w_out, norm2_g, w_up, ffn_dw_k, w_down, final_g):
    bsz = x.shape[0]
    meta = jnp.broadcast_to(meta_tokens[None].astype(x.dtype), (bsz, N_META, D_MODEL))
    h = jnp.concatenate([meta, x], axis=1)
    for i in range(DEPTH):
        hn = rmsnorm(h, norm1_g[i])
        z = hn @ w_in[i]
        a = z[..., :CONV_WIDTH_CH]
        g = z[..., CONV_WIDTH_CH:2 * CONV_WIDTH_CH]
        p = z[..., 2 * CONV_WIDTH_CH:]
        y_conv = conformer_conv_group(a, g, conv_dw_k[i], conv_dw_b[i], conv_ln_g[i], conv_ln_b[i])
        y_pool = multiscale_pool_group(p, pool_w[i], pool_scale[i])
        h = h + jnp.concatenate([y_conv, y_pool], axis=-1) @ w_out[i]
        hn = rmsnorm(h, norm2_g[i])
        ug = causal_dwconv(hn @ w_up[i], ffn_dw_k[i])
        gate, val = ug[..., :D_FF], ug[..., D_FF:]
        h = h + (jax.nn.silu(gate) * val) @ w_down[i]
    return rmsnorm(h, final_g)[:, N_META:]
```

```python
import functools

import jax
import jax.numpy as jnp
from jax import lax
from jax.experimental import pallas as pl
from jax.experimental.pallas import tpu as pltpu

D_MODEL = 1024
N_META = 16
CONV_CH = 512
CONV_HEADS = 8
CONV_HEAD_DIM = CONV_CH // CONV_HEADS
CONV_K = 31
POOL_CH = 512
POOL_WINDOWS = (2, 4, 8, 16)
POOL_GROUP_DIM = POOL_CH // len(POOL_WINDOWS)
IN_COLS = 2 * CONV_CH + POOL_CH
D_FF = 2816
FFN_CONV_K = 3
EPS = 1e-6

SUBLANES = 8
LANES = 128

U_HIST = 32
P_HIST = 16
F_HIST = 8
CONV_ROWS = 32
FF_CHUNKS = (1024, 1024, 768)

SEQ_TILE = 512
VMEM_LIMIT_BYTES = 56 * 1024 * 1024

F32 = jnp.float32
BF16 = jnp.bfloat16


def _rmsnorm(x, g):
    ms = jnp.mean(x * x, axis=-1, keepdims=True)
    return (x * lax.rsqrt(ms + EPS)) * g


def _split_dot(x, w):
    hi = x.astype(BF16)
    lo = (x - hi.astype(F32)).astype(BF16)
    return (jnp.dot(hi, w, preferred_element_type=F32)
            + jnp.dot(lo, w, preferred_element_type=F32))


def _mixer_kernel(h_ref, g1_ref, win_ref, ck_ref, cb_ref, lng_ref, lnb_ref, hm_ref, pw_ref,
                  ps_ref, wout_ref, cu_in_ref, cp_in_ref, *rest, tl, t0, emit_carry):
    if emit_carry:
        hout_ref, cu_out_ref, cp_out_ref, ubuf, pbuf, cbuf = rest
    else:
        hout_ref, ubuf, pbuf, cbuf = rest
    j = pl.program_id(1)

    @pl.when(j == 0)
    def _():
        ubuf[0:U_HIST, :] = cu_in_ref[...]
        pbuf[0:P_HIST, :] = cp_in_ref[...]

    x = h_ref[0]
    hn = _rmsnorm(x, g1_ref[...])
    z = jnp.dot(hn.astype(BF16), win_ref[...], preferred_element_type=F32)
    a = z[:, :CONV_CH]
    g = z[:, CONV_CH:2 * CONV_CH]
    ubuf[U_HIST:U_HIST + tl, :] = a * jax.nn.sigmoid(g)
    pbuf[P_HIST:P_HIST + tl, :] = z[:, 2 * CONV_CH:]

    rows = min(CONV_ROWS, tl)
    base = U_HIST - (CONV_K - 1)

    for r0 in range(0, tl, rows):
        for c in range(CONV_CH // LANES):
            cs = slice(c * LANES, (c + 1) * LANES)
            acc = ck_ref[0:1, cs] * ubuf[r0 + base:r0 + base + rows, cs]
            for k in range(1, CONV_K):
                acc = acc + ck_ref[k:k + 1, cs] * ubuf[r0 + base + k:r0 + base + k + rows, cs]
            cbuf[r0:r0 + rows, cs] = acc + cb_ref[:, cs]

    c = cbuf[...]
    hm = hm_ref[...]
    d = c - _split_dot(c, hm)
    var = _split_dot(d * d, hm)
    yc = (d * lax.rsqrt(var + EPS)) * lng_ref[...] + lnb_ref[...]
    yc = jax.nn.silu(yc)

    ys = []
    for gi, w in enumerate(POOL_WINDOWS):
        cs = slice(gi * POOL_GROUP_DIM, (gi + 1) * POOL_GROUP_DIM)
        tok = pbuf[P_HIST:P_HIST + tl, cs]
        s = tok
        for back in range(1, w):
            s = s + pbuf[P_HIST - back:P_HIST - back + tl, cs]
        if t0 + 1 >= w:
            m = s * (1.0 / w)
        else:
            t = t0 + j * tl + lax.broadcasted_iota(jnp.int32, (tl, 1), 0)
            m = s / jnp.minimum(t + 1, w).astype(F32)
        dg = (m - tok).astype(BF16)
        ys.append(jnp.dot(dg, pw_ref[gi], preferred_element_type=F32))
    yp = jnp.concatenate(ys, axis=-1) * ps_ref[...]

    out = x + jnp.dot(yc.astype(BF16), wout_ref[0:CONV_CH, :], preferred_element_type=F32)
    out = out + jnp.dot(yp.astype(BF16), wout_ref[CONV_CH:, :], preferred_element_type=F32)
    hout_ref[0] = out

    new_u = ubuf[tl:tl + U_HIST, :]
    new_p = pbuf[tl:tl + P_HIST, :]
    ubuf[0:U_HIST, :] = new_u
    pbuf[0:P_HIST, :] = new_p
    if emit_carry:
        cu_out_ref[...] = new_u
        cp_out_ref[...] = new_p


def _ffn_kernel(h_ref, g2_ref, wup_ref, fk_ref, wdn_ref, cf_in_ref, *rest, tl,
                final_norm, emit_carry):
    rest = list(rest)
    fg_ref = rest.pop(0) if final_norm else None
    hout_ref = rest.pop(0)
    cf_out_ref = rest.pop(0) if emit_carry else None
    hist, gbuf, vbuf = rest
    j = pl.program_id(1)

    @pl.when(j == 0)
    def _():
        hist[...] = cf_in_ref[...]

    x = h_ref[0]
    hn = _rmsnorm(x, g2_ref[...]).astype(BF16)
    acc = x
    lo = 0
    for width in FF_CHUNKS:
        conv = []
        for buf, off in ((gbuf, lo), (vbuf, D_FF + lo)):
            cols = slice(off, off + width)
            buf[0:F_HIST, 0:width] = hist[:, cols]
            buf[F_HIST:F_HIST + tl, 0:width] = jnp.dot(
                hn, wup_ref[:, cols], preferred_element_type=F32)
            hist[:, cols] = buf[tl:tl + F_HIST, 0:width]
            y = fk_ref[0:1, cols] * buf[F_HIST - 2:F_HIST - 2 + tl, 0:width]
            y = y + fk_ref[1:2, cols] * buf[F_HIST - 1:F_HIST - 1 + tl, 0:width]
            y = y + fk_ref[2:3, cols] * buf[F_HIST:F_HIST + tl, 0:width]
            conv.append(y)
        act = (jax.nn.silu(conv[0]) * conv[1]).astype(BF16)
        acc = acc + jnp.dot(act, wdn_ref[lo:lo + width, :], preferred_element_type=F32)
        lo += width
    if final_norm:
        acc = _rmsnorm(acc, fg_ref[...])
    hout_ref[0] = acc
    if emit_carry:
        cf_out_ref[...] = hist[...]


def _const_spec(shape):
    zeros = (0,) * len(shape)
    return pl.BlockSpec(shape, lambda b, j: zeros, pipeline_mode=pl.Buffered(1))


def _seq_spec(tl):
    return pl.BlockSpec((1, tl, D_MODEL), lambda b, j: (b, j, 0))


def _compiler_params():
    return pltpu.CompilerParams(
        dimension_semantics=("arbitrary", "arbitrary"),
        vmem_limit_bytes=VMEM_LIMIT_BYTES)


def _mixer(h, p, cu, cp, *, t0, emit_carry):
    bsz, length, _ = h.shape
    tl = min(SEQ_TILE, length)
    consts = (p["g1"], p["w_in"], p["ck"], p["cb"], p["ln_g"], p["ln_b"], p["hm"], p["pw"],
              p["ps"], p["w_out"], cu, cp)
    out_shape = [jax.ShapeDtypeStruct(h.shape, F32)]
    out_specs = [_seq_spec(tl)]
    if emit_carry:
        out_shape += [jax.ShapeDtypeStruct(cu.shape, F32), jax.ShapeDtypeStruct(cp.shape, F32)]
        out_specs += [pl.BlockSpec(cu.shape, lambda b, j: (0, 0)),
                      pl.BlockSpec(cp.shape, lambda b, j: (0, 0))]
    return pl.pallas_call(
        functools.partial(_mixer_kernel, tl=tl, t0=t0, emit_carry=emit_carry),
        grid=(bsz, length // tl),
        in_specs=[_seq_spec(tl)] + [_const_spec(c.shape) for c in consts],
        out_specs=out_specs,
        out_shape=out_shape,
        scratch_shapes=[pltpu.VMEM((U_HIST + tl, CONV_CH), F32),
                        pltpu.VMEM((P_HIST + tl, POOL_CH), F32),
                        pltpu.VMEM((tl, CONV_CH), F32)],
        compiler_params=_compiler_params(),
        name="mixer_meta" if emit_carry else "mixer",
    )(h, *consts)


def _ffn(h, p, cf, *, final_g, emit_carry):
    bsz, length, _ = h.shape
    tl = min(SEQ_TILE, length)
    final_norm = final_g is not None
    consts = (p["g2"], p["w_up"], p["fk"], p["w_down"], cf) + ((final_g,) if final_norm else ())
    out_shape = [jax.ShapeDtypeStruct(h.shape, F32)]
    out_specs = [_seq_spec(tl)]
    if emit_carry:
        out_shape += [jax.ShapeDtypeStruct(cf.shape, F32)]
        out_specs += [pl.BlockSpec(cf.shape, lambda b, j: (0, 0))]
    return pl.pallas_call(
        functools.partial(_ffn_kernel, tl=tl, final_norm=final_norm, emit_carry=emit_carry),
        grid=(bsz, length // tl),
        in_specs=[_seq_spec(tl)] + [_const_spec(c.shape) for c in consts],
        out_specs=out_specs,
        out_shape=out_shape,
        scratch_shapes=[pltpu.VMEM((F_HIST, 2 * D_FF), F32),
                        pltpu.VMEM((F_HIST + tl, max(FF_CHUNKS)), F32),
                        pltpu.VMEM((F_HIST + tl, max(FF_CHUNKS)), F32)],
        compiler_params=_compiler_params(),
        name="ffn_meta" if emit_carry else "ffn",
    )(h, *consts)


def _head_mean_matrix():
    head = jnp.arange(CONV_CH) // CONV_HEAD_DIM
    return jnp.where(head[:, None] == head[None, :], 1.0 / CONV_HEAD_DIM, 0.0).astype(BF16)


def kernel(x, meta_tokens, norm1_g, w_in, conv_dw_k, conv_dw_b, conv_ln_g, conv_ln_b, pool_w,
           pool_scale, w_out, norm2_g, w_up, ffn_dw_k, w_down, final_g):
    assert x.ndim == 3 and x.shape[2] == D_MODEL and x.shape[1] % SEQ_TILE == 0
    assert meta_tokens.shape == (N_META, D_MODEL)
    depth = w_in.shape[0]
    hm = _head_mean_matrix()
    row = lambda v: v.reshape(1, -1).astype(F32)

    hx = x.astype(F32)
    hmeta = meta_tokens.astype(F32)[None]
    for i in range(depth):
        p = {
            "g1": row(norm1_g[i]), "w_in": w_in[i].astype(BF16), "ck": conv_dw_k[i].astype(F32),
            "cb": row(conv_dw_b[i]), "ln_g": row(conv_ln_g[i]), "ln_b": row(conv_ln_b[i]),
            "hm": hm, "pw": pool_w[i].astype(BF16), "ps": row(pool_scale[i]),
            "w_out": w_out[i].astype(BF16), "g2": row(norm2_g[i]),
            "w_up": w_up[i].astype(BF16), "fk": ffn_dw_k[i].astype(F32),
            "w_down": w_down[i].astype(BF16),
        }
        last = i == depth - 1
        zero_u = jnp.zeros((U_HIST, CONV_CH), F32)
        zero_p = jnp.zeros((P_HIST, POOL_CH), F32)
        zero_f = jnp.zeros((F_HIST, 2 * D_FF), F32)
        hmeta, cu, cp = _mixer(hmeta, p, zero_u, zero_p, t0=0, emit_carry=True)
        (hx,) = _mixer(hx, p, cu, cp, t0=N_META, emit_carry=False)
        hmeta, cf = _ffn(hmeta, p, zero_f, final_g=None, emit_carry=True)
        (hx,) = _ffn(hx, p, cf, final_g=row(final_g) if last else None, emit_carry=False)
    return hx.astype(x.dtype)
```

```python
import functools

import jax
import jax.numpy as jnp
from jax import lax
from jax.experimental import pallas as pl
from jax.experimental.pallas import tpu as pltpu

D_MODEL = 1024
N_META = 16
CONV_CH = 512
CONV_HEADS = 8
CONV_HEAD_DIM = CONV_CH // CONV_HEADS
CONV_K = 31
POOL_CH = 512
POOL_WINDOWS = (2, 4, 8, 16)
POOL_GROUP_DIM = POOL_CH // len(POOL_WINDOWS)
IN_COLS = 2 * CONV_CH + POOL_CH
D_FF = 2816
FFN_CONV_K = 3
EPS = 1e-6

SUBLANES = 8
LANES = 128
MXU_DIM = 256

U_HIST = 32
P_HIST = 16
F_HIST = 8
ROW_PITCH = 2
CONV_ROWS = 64
FF_CHUNKS = (1024, 1024, 768)

SEQ_TILE = 512
VMEM_LIMIT_BYTES = 56 * 1024 * 1024

F32 = jnp.float32
BF16 = jnp.bfloat16


def _rmsnorm(x, g):
    ms = jnp.mean(x * x, axis=-1, keepdims=True)
    return (x * lax.rsqrt(ms + EPS)) * g


def _split_dot(x, w):
    hi = x.astype(BF16)
    lo = (x - hi.astype(F32)).astype(BF16)
    return (jnp.dot(hi, w, preferred_element_type=F32)
            + jnp.dot(lo, w, preferred_element_type=F32))


def _tail(old, new, n):
    m = new.shape[0]
    if m >= n:
        return new[m - n:]
    return jnp.concatenate([old[m:], new], axis=0)


def _every_other_row(start, size):
    return pl.ds(ROW_PITCH * start, size, stride=ROW_PITCH)


def _mixer_kernel(h_ref, g1_ref, win_ref, ck_ref, cb_ref, lng_ref, lnb_ref, hm_ref, pw_ref,
                  ps_ref, wout_ref, cu_in_ref, cp_in_ref, *rest, tl, t0, emit_carry):
    rest = list(rest)
    hout_ref = rest.pop(0)
    cu_out_ref = rest.pop(0) if emit_carry else None
    cp_out_ref = rest.pop(0) if emit_carry else None
    uhist, phist, ubuf, pbuf = rest
    j = pl.program_id(1)

    @pl.when(j == 0)
    def _():
        uhist[...] = cu_in_ref[...]
        phist[...] = cp_in_ref[...]

    x = h_ref[0]
    hn = _rmsnorm(x, g1_ref[...])
    z = jnp.dot(hn.astype(BF16), win_ref[...], preferred_element_type=F32)
    u = z[:, :CONV_CH] * jax.nn.sigmoid(z[:, CONV_CH:2 * CONV_CH])
    p = z[:, 2 * CONV_CH:]

    for c in range(CONV_CH // LANES):
        cs = slice(c * LANES, (c + 1) * LANES)
        ubuf[c, _every_other_row(0, U_HIST), :] = uhist[:, cs]
        ubuf[c, _every_other_row(U_HIST, tl), :] = u[:, cs]
        pbuf[c, _every_other_row(0, P_HIST), :] = phist[:, cs]
        pbuf[c, _every_other_row(P_HIST, tl), :] = p[:, cs]
    uhist[...] = _tail(uhist[...], u, U_HIST)
    phist[...] = _tail(phist[...], p, P_HIST)

    rows = min(CONV_ROWS, tl)
    base = U_HIST - (CONV_K - 1)
    cols = []
    for c in range(CONV_CH // LANES):
        cs = slice(c * LANES, (c + 1) * LANES)
        pieces = []
        for r0 in range(0, tl, rows):
            acc = ck_ref[0:1, cs] * ubuf[c, _every_other_row(r0 + base, rows), :]
            for k in range(1, CONV_K):
                acc = acc + ck_ref[k:k + 1, cs] * ubuf[c, _every_other_row(r0 + base + k, rows), :]
            pieces.append(acc)
        cols.append(jnp.concatenate(pieces, axis=0))
    cv = jnp.concatenate(cols, axis=1) + cb_ref[...]

    hm = hm_ref[...]
    mean = jnp.concatenate(
        [_split_dot(cv[:, i:i + MXU_DIM], hm) for i in range(0, CONV_CH, MXU_DIM)], axis=1)
    d = cv - mean
    var = jnp.concatenate(
        [_split_dot(d[:, i:i + MXU_DIM] * d[:, i:i + MXU_DIM], hm)
         for i in range(0, CONV_CH, MXU_DIM)], axis=1)
    yc = (d * lax.rsqrt(var + EPS)) * lng_ref[...] + lnb_ref[...]
    yc = jax.nn.silu(yc)

    ys = []
    for gi, w in enumerate(POOL_WINDOWS):
        cs = slice(gi * POOL_GROUP_DIM, (gi + 1) * POOL_GROUP_DIM)
        tok = p[:, cs]
        s = tok
        for back in range(1, w):
            s = s + pbuf[gi, _every_other_row(P_HIST - back, tl), :]
        if t0 + 1 >= w:
            m = s * (1.0 / w)
        else:
            t = t0 + j * tl + lax.broadcasted_iota(jnp.int32, (tl, 1), 0)
            m = s / jnp.minimum(t + 1, w).astype(F32)
        dg = (m - tok).astype(BF16)
        ys.append(jnp.dot(dg, pw_ref[gi], preferred_element_type=F32))
    yp = jnp.concatenate(ys, axis=-1) * ps_ref[...]

    out = x + jnp.dot(yc.astype(BF16), wout_ref[0:CONV_CH, :], preferred_element_type=F32)
    out = out + jnp.dot(yp.astype(BF16), wout_ref[CONV_CH:, :], preferred_element_type=F32)
    hout_ref[0] = out
    if emit_carry:
        cu_out_ref[...] = uhist[...]
        cp_out_ref[...] = phist[...]


def _ffn_kernel(h_ref, g2_ref, wup_ref, fk_ref, wdn_ref, cf_in_ref, *rest, tl,
                final_norm, emit_carry):
    rest = list(rest)
    fg_ref = rest.pop(0) if final_norm else None
    hout_ref = rest.pop(0)
    cf_out_ref = rest.pop(0) if emit_carry else None
    hist, gbuf, vbuf = rest
    j = pl.program_id(1)

    @pl.when(j == 0)
    def _():
        hist[...] = cf_in_ref[...]

    x = h_ref[0]
    hn = _rmsnorm(x, g2_ref[...]).astype(BF16)
    acc = x
    lo = 0
    for width in FF_CHUNKS:
        conv = []
        for buf, off in ((gbuf, lo), (vbuf, D_FF + lo)):
            up = jnp.dot(hn, wup_ref[:, off:off + width], preferred_element_type=F32)
            ys = []
            for s in range(width // LANES):
                sl = slice(s * LANES, (s + 1) * LANES)
                cols = slice(off + s * LANES, off + (s + 1) * LANES)
                buf[s, _every_other_row(0, F_HIST), :] = hist[:, cols]
                buf[s, _every_other_row(F_HIST, tl), :] = up[:, sl]
                y = fk_ref[0:1, cols] * buf[s, _every_other_row(F_HIST - 2, tl), :]
                y = y + fk_ref[1:2, cols] * buf[s, _every_other_row(F_HIST - 1, tl), :]
                ys.append(y + fk_ref[2:3, cols] * up[:, sl])
            hist[:, off:off + width] = _tail(hist[:, off:off + width], up, F_HIST)
            conv.append(jnp.concatenate(ys, axis=1))
        act = (jax.nn.silu(conv[0]) * conv[1]).astype(BF16)
        acc = acc + jnp.dot(act, wdn_ref[lo:lo + width, :], preferred_element_type=F32)
        lo += width
    if final_norm:
        acc = _rmsnorm(acc, fg_ref[...])
    hout_ref[0] = acc
    if emit_carry:
        cf_out_ref[...] = hist[...]


def _const_spec(shape):
    zeros = (0,) * len(shape)
    return pl.BlockSpec(shape, lambda b, j: zeros, pipeline_mode=pl.Buffered(1))


def _seq_spec(tl):
    return pl.BlockSpec((1, tl, D_MODEL), lambda b, j: (b, j, 0))


def _compiler_params():
    return pltpu.CompilerParams(
        dimension_semantics=("arbitrary", "arbitrary"),
        vmem_limit_bytes=VMEM_LIMIT_BYTES)


def _mixer(h, p, cu, cp, *, t0, emit_carry):
    bsz, length, _ = h.shape
    tl = min(SEQ_TILE, length)
    consts = (p["g1"], p["w_in"], p["ck"], p["cb"], p["ln_g"], p["ln_b"], p["hm"], p["pw"],
              p["ps"], p["w_out"], cu, cp)
    out_shape = [jax.ShapeDtypeStruct(h.shape, F32)]
    out_specs = [_seq_spec(tl)]
    if emit_carry:
        out_shape += [jax.ShapeDtypeStruct(cu.shape, F32), jax.ShapeDtypeStruct(cp.shape, F32)]
        out_specs += [pl.BlockSpec(cu.shape, lambda b, j: (0, 0)),
                      pl.BlockSpec(cp.shape, lambda b, j: (0, 0))]
    return pl.pallas_call(
        functools.partial(_mixer_kernel, tl=tl, t0=t0, emit_carry=emit_carry),
        grid=(bsz, length // tl),
        in_specs=[_seq_spec(tl)] + [_const_spec(c.shape) for c in consts],
        out_specs=out_specs,
        out_shape=out_shape,
        scratch_shapes=[
            pltpu.VMEM((U_HIST, CONV_CH), F32),
            pltpu.VMEM((P_HIST, POOL_CH), F32),
            pltpu.VMEM((CONV_CH // LANES, ROW_PITCH * (U_HIST + tl), LANES), F32),
            pltpu.VMEM((POOL_CH // LANES, ROW_PITCH * (P_HIST + tl), LANES), F32)],
        compiler_params=_compiler_params(),
        name="mixer_meta" if emit_carry else "mixer",
    )(h, *consts)


def _ffn(h, p, cf, *, final_g, emit_carry):
    bsz, length, _ = h.shape
    tl = min(SEQ_TILE, length)
    final_norm = final_g is not None
    consts = (p["g2"], p["w_up"], p["fk"], p["w_down"], cf) + ((final_g,) if final_norm else ())
    out_shape = [jax.ShapeDtypeStruct(h.shape, F32)]
    out_specs = [_seq_spec(tl)]
    if emit_carry:
        out_shape += [jax.ShapeDtypeStruct(cf.shape, F32)]
        out_specs += [pl.BlockSpec(cf.shape, lambda b, j: (0, 0))]
    seq_buf = pltpu.VMEM((max(FF_CHUNKS) // LANES, ROW_PITCH * (F_HIST + tl), LANES), F32)
    return pl.pallas_call(
        functools.partial(_ffn_kernel, tl=tl, final_norm=final_norm, emit_carry=emit_carry),
        grid=(bsz, length // tl),
        in_specs=[_seq_spec(tl)] + [_const_spec(c.shape) for c in consts],
        out_specs=out_specs,
        out_shape=out_shape,
        scratch_shapes=[pltpu.VMEM((F_HIST, 2 * D_FF), F32), seq_buf, seq_buf],
        compiler_params=_compiler_params(),
        name="ffn_meta" if emit_carry else "ffn",
    )(h, *consts)


def _head_mean_matrix():
    head = jnp.arange(MXU_DIM) // CONV_HEAD_DIM
    return jnp.where(head[:, None] == head[None, :], 1.0 / CONV_HEAD_DIM, 0.0).astype(BF16)


def kernel(x, meta_tokens, norm1_g, w_in, conv_dw_k, conv_dw_b, conv_ln_g, conv_ln_b, pool_w,
           pool_scale, w_out, norm2_g, w_up, ffn_dw_k, w_down, final_g):
    assert x.ndim == 3 and x.shape[2] == D_MODEL and x.shape[1] % SEQ_TILE == 0
    assert meta_tokens.shape == (N_META, D_MODEL)
    depth = w_in.shape[0]
    hm = _head_mean_matrix()
    row = lambda v: v.reshape(1, -1).astype(F32)

    hx = x.astype(F32)
    hmeta = meta_tokens.astype(F32)[None]
    for i in range(depth):
        p = {
            "g1": row(norm1_g[i]), "w_in": w_in[i].astype(BF16), "ck": conv_dw_k[i].astype(F32),
            "cb": row(conv_dw_b[i]), "ln_g": row(conv_ln_g[i]), "ln_b": row(conv_ln_b[i]),
            "hm": hm, "pw": pool_w[i].astype(BF16), "ps": row(pool_scale[i]),
            "w_out": w_out[i].astype(BF16), "g2": row(norm2_g[i]),
            "w_up": w_up[i].astype(BF16), "fk": ffn_dw_k[i].astype(F32),
            "w_down": w_down[i].astype(BF16),
        }
        last = i == depth - 1
        zero_u = jnp.zeros((U_HIST, CONV_CH), F32)
        zero_p = jnp.zeros((P_HIST, POOL_CH), F32)
        zero_f = jnp.zeros((F_HIST, 2 * D_FF), F32)
        hmeta, cu, cp = _mixer(hmeta, p, zero_u, zero_p, t0=0, emit_carry=True)
        (hx,) = _mixer(hx, p, cu, cp, t0=N_META, emit_carry=False)
        hmeta, cf = _ffn(hmeta, p, zero_f, final_g=None, emit_carry=True)
        (hx,) = _ffn(hx, p, cf, final_g=row(final_g) if last else None, emit_carry=False)
    return hx.astype(x.dtype)
```

```python
import functools

import jax
import jax.numpy as jnp
from jax import lax
from jax.experimental import pallas as pl
from jax.experimental.pallas import tpu as pltpu

D_MODEL = 1024
N_META = 16
CONV_CH = 512
CONV_HEADS = 8
CONV_HEAD_DIM = CONV_CH // CONV_HEADS
CONV_K = 31
POOL_CH = 512
POOL_WINDOWS = (2, 4, 8, 16)
POOL_GROUP_DIM = POOL_CH // len(POOL_WINDOWS)
IN_COLS = 2 * CONV_CH + POOL_CH
D_FF = 2816
FFN_CONV_K = 3
EPS = 1e-6

SUBLANES = 8
LANES = 128
MXU_DIM = 256

U_HIST = 32
P_HIST = 16
F_HIST = 8
ROW_PITCH = 2
CONV_ROWS = 64
FF_CHUNKS = (512, 512, 512, 512, 512, 256)

SEQ_TILE = 512
VMEM_LIMIT_BYTES = 60 * 1024 * 1024

F32 = jnp.float32
BF16 = jnp.bfloat16


def _rmsnorm(x, g):
    ms = jnp.mean(x * x, axis=-1, keepdims=True)
    return (x * lax.rsqrt(ms + EPS)) * g


def _split_dot(x, w):
    hi = x.astype(BF16)
    lo = (x - hi.astype(F32)).astype(BF16)
    return (jnp.dot(hi, w, preferred_element_type=F32)
            + jnp.dot(lo, w, preferred_element_type=F32))


def _tail(old, new, n):
    m = new.shape[0]
    if m >= n:
        return new[m - n:]
    return jnp.concatenate([old[m:], new], axis=0)


def _every_other_row(start, size):
    return pl.ds(ROW_PITCH * start, size, stride=ROW_PITCH)


def _mixer_in(x, g1_ref, win_ref, uhist, phist, ubuf, pbuf):
    tl = x.shape[0]
    hn = _rmsnorm(x, g1_ref[...])
    z = jnp.dot(hn.astype(BF16), win_ref[...], preferred_element_type=F32)
    u = z[:, :CONV_CH] * jax.nn.sigmoid(z[:, CONV_CH:2 * CONV_CH])
    p = z[:, 2 * CONV_CH:]

    for c in range(CONV_CH // LANES):
        cs = slice(c * LANES, (c + 1) * LANES)
        ubuf[c, _every_other_row(0, U_HIST), :] = uhist[:, cs]
        ubuf[c, _every_other_row(U_HIST, tl), :] = u[:, cs]
        pbuf[c, _every_other_row(0, P_HIST), :] = phist[:, cs]
        pbuf[c, _every_other_row(P_HIST, tl), :] = p[:, cs]
    uhist[...] = _tail(uhist[...], u, U_HIST)
    phist[...] = _tail(phist[...], p, P_HIST)
    return p


def _conv_piece(ck_ref, ubuf, c, r0, rows):
    cs = slice(c * LANES, (c + 1) * LANES)
    base = U_HIST - (CONV_K - 1)
    acc = ck_ref[0:1, cs] * ubuf[c, _every_other_row(r0 + base, rows), :]
    for k in range(1, CONV_K):
        acc = acc + ck_ref[k:k + 1, cs] * ubuf[c, _every_other_row(r0 + base + k, rows), :]
    return acc


def _mixer_out(x, p, pieces, first_row, cb_ref, lng_ref, lnb_ref, hm_ref, pw_ref, ps_ref, wout_ref,
               pbuf, *, full_windows):
    tl = x.shape[0]
    cv = jnp.concatenate([jnp.concatenate(pc, axis=0) for pc in pieces], axis=1) + cb_ref[...]

    hm = hm_ref[...]
    mean = jnp.concatenate(
        [_split_dot(cv[:, i:i + MXU_DIM], hm) for i in range(0, CONV_CH, MXU_DIM)], axis=1)
    d = cv - mean
    var = jnp.concatenate(
        [_split_dot(d[:, i:i + MXU_DIM] * d[:, i:i + MXU_DIM], hm)
         for i in range(0, CONV_CH, MXU_DIM)], axis=1)
    yc = (d * lax.rsqrt(var + EPS)) * lng_ref[...] + lnb_ref[...]
    yc = jax.nn.silu(yc)

    ys = []
    for gi, w in enumerate(POOL_WINDOWS):
        cs = slice(gi * POOL_GROUP_DIM, (gi + 1) * POOL_GROUP_DIM)
        tok = p[:, cs]
        s = tok
        for back in range(1, w):
            s = s + pbuf[gi, _every_other_row(P_HIST - back, tl), :]
        if full_windows:
            m = s * (1.0 / w)
        else:
            t = first_row + lax.broadcasted_iota(jnp.int32, (tl, 1), 0)
            m = s / jnp.minimum(t + 1, w).astype(F32)
        dg = (m - tok).astype(BF16)
        ys.append(jnp.dot(dg, pw_ref[gi], preferred_element_type=F32))
    yp = jnp.concatenate(ys, axis=-1) * ps_ref[...]

    out = x + jnp.dot(yc.astype(BF16), wout_ref[0:CONV_CH, :], preferred_element_type=F32)
    return out + jnp.dot(yp.astype(BF16), wout_ref[CONV_CH:, :], preferred_element_type=F32)


def _ffn_stage(x, g2_ref, wup_ref, fk_ref, wdn_ref, fg_ref, fhist, fbuf, after_chunk):
    tl = x.shape[0]
    hn = _rmsnorm(x, g2_ref[...]).astype(BF16)
    starts = [sum(FF_CHUNKS[:i]) for i in range(len(FF_CHUNKS))]

    def up_proj(ci):
        ups = []
        for half, base in enumerate((0, D_FF)):
            off, width = base + starts[ci], FF_CHUNKS[ci]
            up = jnp.dot(hn, wup_ref[:, off:off + width], preferred_element_type=F32)
            for s in range(width // LANES):
                cols = slice(off + s * LANES, off + (s + 1) * LANES)
                buf = fbuf.at[ci % 2, half, s]
                buf[_every_other_row(0, F_HIST), :] = fhist[:, cols]
                buf[_every_other_row(F_HIST, tl), :] = up[:, s * LANES:(s + 1) * LANES]
            fhist[:, off:off + width] = _tail(fhist[:, off:off + width], up, F_HIST)
            ups.append(up)
        return ups

    def gated(ci, ups):
        conv = []
        for half, base in enumerate((0, D_FF)):
            off, width = base + starts[ci], FF_CHUNKS[ci]
            ys = []
            for s in range(width // LANES):
                cols = slice(off + s * LANES, off + (s + 1) * LANES)
                buf = fbuf.at[ci % 2, half, s]
                y = fk_ref[0:1, cols] * buf[_every_other_row(F_HIST - 2, tl), :]
                y = y + fk_ref[1:2, cols] * buf[_every_other_row(F_HIST - 1, tl), :]
                ys.append(y + fk_ref[2:3, cols] * ups[half][:, s * LANES:(s + 1) * LANES])
            conv.append(jnp.concatenate(ys, axis=1))
        return (jax.nn.silu(conv[0]) * conv[1]).astype(BF16)

    acc = x
    ups = up_proj(0)
    for ci, width in enumerate(FF_CHUNKS):
        nxt = up_proj(ci + 1) if ci + 1 < len(FF_CHUNKS) else None
        act = gated(ci, ups)
        acc = acc + jnp.dot(act, wdn_ref[starts[ci]:starts[ci] + width, :],
                            preferred_element_type=F32)
        after_chunk(ci)
        ups = nxt
    if fg_ref is not None:
        acc = _rmsnorm(acc, fg_ref[...])
    return acc


def _layer_kernel(h_ref, g1_ref, win_ref, ck_ref, cb_ref, lng_ref, lnb_ref, hm_ref, pw_ref, ps_ref,
                  wout_ref, g2_ref, wup_ref, fk_ref, wdn_ref, cu_in_ref, cp_in_ref, cf_in_ref,
                  *rest, tl, nt, t0, final_norm, emit_carry):
    rest = list(rest)
    fg_ref = rest.pop(0) if final_norm else None
    out_ref = rest.pop(0)
    carry_out = [rest.pop(0) for _ in range(3)] if emit_carry else None
    uhist, phist, ubuf, pbuf, fhist, fbuf, hmid = rest
    g = pl.program_id(0)
    tile = lax.rem(g, nt)

    @pl.when(tile == 0)
    def _():
        uhist[...] = cu_in_ref[...]
        phist[...] = cp_in_ref[...]

    @pl.when(g == 0)
    def _():
        hmid[1] = jnp.zeros((tl, D_MODEL), F32)

    @pl.when(jnp.logical_or(g == 0, lax.rem(g + nt - 1, nt) == 0))
    def _():
        fhist[...] = cf_in_ref[...]

    xm = h_ref[0]
    p = _mixer_in(xm, g1_ref, win_ref, uhist, phist, ubuf, pbuf)

    rows = min(CONV_ROWS, tl)
    slabs = CONV_CH // LANES
    jobs = [(c, r0) for r0 in range(0, tl, rows) for c in range(slabs)]
    pieces = [[] for _ in range(slabs)]

    def conv_share(ci):
        left = len(FF_CHUNKS) - ci
        for _ in range(-(-len(jobs) // left)):
            c, r0 = jobs.pop(0)
            pieces[c].append(_conv_piece(ck_ref, ubuf, c, r0, rows))

    out_ref[0] = _ffn_stage(hmid[lax.rem(g + 1, 2)], g2_ref, wup_ref, fk_ref, wdn_ref, fg_ref,
                            fhist, fbuf, conv_share)

    hmid[lax.rem(g, 2)] = _mixer_out(
        xm, p, pieces, t0 + tile * tl, cb_ref, lng_ref, lnb_ref, hm_ref, pw_ref, ps_ref, wout_ref,
        pbuf, full_windows=t0 + 1 >= max(POOL_WINDOWS))

    if emit_carry:
        carry_out[0][...] = uhist[...]
        carry_out[1][...] = phist[...]
        carry_out[2][...] = fhist[...]


def _layer_spec(arr, layer):
    zeros = (0,) * (arr.ndim - 1)
    return pl.BlockSpec((None,) + arr.shape[1:], lambda g: (layer,) + zeros,
                        pipeline_mode=pl.Buffered(1))


def _whole_spec(arr):
    zeros = (0,) * arr.ndim
    return pl.BlockSpec(arr.shape, lambda g: zeros, pipeline_mode=pl.Buffered(1))


def _layer(h, params, layer, carry, *, t0, final_g, emit_carry):
    bsz, length, _ = h.shape
    tl = min(SEQ_TILE, length)
    nt = length // tl
    steps = bsz * nt
    final_norm = final_g is not None

    def in_tile(g):
        t = jnp.minimum(g, steps - 1)
        return (t // nt, t % nt, 0)

    def out_tile(g):
        t = jnp.maximum(g - 1, 0)
        return (t // nt, t % nt, 0)

    operands = [h] + list(params) + list(carry) + ([final_g] if final_norm else [])
    in_specs = ([pl.BlockSpec((1, tl, D_MODEL), in_tile)]
                + [_layer_spec(a, layer) for a in params]
                + [_whole_spec(a) for a in carry]
                + ([_whole_spec(final_g)] if final_norm else []))
    out_shape = [jax.ShapeDtypeStruct(h.shape, F32)]
    out_specs = [pl.BlockSpec((1, tl, D_MODEL), out_tile)]
    if emit_carry:
        out_shape += [jax.ShapeDtypeStruct(c.shape, F32) for c in carry]
        out_specs += [pl.BlockSpec(c.shape, lambda g: (0, 0)) for c in carry]
    ff_buf = pltpu.VMEM(
        (2, 2, max(FF_CHUNKS) // LANES, ROW_PITCH * (F_HIST + tl), LANES), F32)
    return pl.pallas_call(
        functools.partial(_layer_kernel, tl=tl, nt=nt, t0=t0, final_norm=final_norm,
                          emit_carry=emit_carry),
        grid=(steps + 1,),
        in_specs=in_specs,
        out_specs=out_specs,
        out_shape=out_shape,
        scratch_shapes=[
            pltpu.VMEM((U_HIST, CONV_CH), F32),
            pltpu.VMEM((P_HIST, POOL_CH), F32),
            pltpu.VMEM((CONV_CH // LANES, ROW_PITCH * (U_HIST + tl), LANES), F32),
            pltpu.VMEM((POOL_CH // LANES, ROW_PITCH * (P_HIST + tl), LANES), F32),
            pltpu.VMEM((F_HIST, 2 * D_FF), F32),
            ff_buf,
            pltpu.VMEM((2, tl, D_MODEL), F32)],
        compiler_params=pltpu.CompilerParams(
            dimension_semantics=("arbitrary",), vmem_limit_bytes=VMEM_LIMIT_BYTES),
        name="layer_meta" if emit_carry else "layer",
    )(*operands)


def _head_mean_matrix(depth):
    head = jnp.arange(MXU_DIM) // CONV_HEAD_DIM
    hm = jnp.where(head[:, None] == head[None, :], 1.0 / CONV_HEAD_DIM, 0.0).astype(BF16)
    return jnp.broadcast_to(hm, (depth, MXU_DIM, MXU_DIM))


def kernel(x, meta_tokens, norm1_g, w_in, conv_dw_k, conv_dw_b, conv_ln_g, conv_ln_b, pool_w,
           pool_scale, w_out, norm2_g, w_up, ffn_dw_k, w_down, final_g):
    assert x.ndim == 3 and x.shape[2] == D_MODEL and x.shape[1] % SEQ_TILE == 0
    assert meta_tokens.shape == (N_META, D_MODEL)
    depth = w_in.shape[0]
    rows = lambda v: v.reshape(depth, 1, -1).astype(F32)
    params = (rows(norm1_g), w_in.astype(BF16), conv_dw_k.astype(F32), rows(conv_dw_b),
              rows(conv_ln_g), rows(conv_ln_b), _head_mean_matrix(depth), pool_w.astype(BF16),
              rows(pool_scale), w_out.astype(BF16), rows(norm2_g), w_up.astype(BF16),
              ffn_dw_k.astype(F32), w_down.astype(BF16))
    zero_carry = (jnp.zeros((U_HIST, CONV_CH), F32), jnp.zeros((P_HIST, POOL_CH), F32),
                  jnp.zeros((F_HIST, 2 * D_FF), F32))

    hx = x.astype(F32)
    hmeta = meta_tokens.astype(F32)[None]
    for i in range(depth):
        fg = final_g.reshape(1, -1).astype(F32) if i == depth - 1 else None
        hmeta, *carry = _layer(hmeta, params, i, zero_carry, t0=0, final_g=None, emit_carry=True)
        (hx,) = _layer(hx, params, i, carry, t0=N_META, final_g=fg, emit_carry=False)
    return hx.astype(x.dtype)
```

```python
import functools

import jax
import jax.numpy as jnp
from jax import lax
from jax.experimental import pallas as pl
from jax.experimental.pallas import tpu as pltpu

D_MODEL = 1024
N_META = 16
CONV_CH = 512
CONV_HEADS = 8
CONV_HEAD_DIM = CONV_CH // CONV_HEADS
CONV_K = 31
POOL_CH = 512
POOL_WINDOWS = (2, 4, 8, 16)
POOL_GROUP_DIM = POOL_CH // len(POOL_WINDOWS)
IN_COLS = 2 * CONV_CH + POOL_CH
D_FF = 2816
FFN_CONV_K = 3
EPS = 1e-6

SUBLANES = 8
LANES = 128
MXU_DIM = 256

U_HIST = 32
P_HIST = 16
F_HIST = 8
ROW_PITCH = 2
CONV_ROWS = 64
FF_CHUNKS = (768, 768, 768, 512)

SEQ_TILE = 512
VMEM_LIMIT_BYTES = 60 * 1024 * 1024

F32 = jnp.float32
BF16 = jnp.bfloat16


def _rmsnorm(x, g):
    ms = jnp.mean(x * x, axis=-1, keepdims=True)
    return (x * lax.rsqrt(ms + EPS)) * g


def _split_dot(x, w):
    hi = x.astype(BF16)
    lo = (x - hi.astype(F32)).astype(BF16)
    return (jnp.dot(hi, w, preferred_element_type=F32)
            + jnp.dot(lo, w, preferred_element_type=F32))


def _tail(old, new, n):
    m = new.shape[0]
    if m >= n:
        return new[m - n:]
    return jnp.concatenate([old[m:], new], axis=0)


def _every_other_row(start, size):
    return pl.ds(ROW_PITCH * start, size, stride=ROW_PITCH)


def _mixer_in(x, g1_ref, win_ref, uhist, phist, ubuf, pbuf):
    tl = x.shape[0]
    hn = _rmsnorm(x, g1_ref[...])
    z = jnp.dot(hn.astype(BF16), win_ref[...], preferred_element_type=F32)
    u = z[:, :CONV_CH] * jax.nn.sigmoid(z[:, CONV_CH:2 * CONV_CH])
    p = z[:, 2 * CONV_CH:]

    for c in range(CONV_CH // LANES):
        cs = slice(c * LANES, (c + 1) * LANES)
        ubuf[c, _every_other_row(0, U_HIST), :] = uhist[:, cs]
        ubuf[c, _every_other_row(U_HIST, tl), :] = u[:, cs]
        pbuf[c, _every_other_row(0, P_HIST), :] = phist[:, cs]
        pbuf[c, _every_other_row(P_HIST, tl), :] = p[:, cs]
    uhist[...] = _tail(uhist[...], u, U_HIST)
    phist[...] = _tail(phist[...], p, P_HIST)


def _conv_piece(ck_ref, ubuf, c, r0, rows):
    cs = slice(c * LANES, (c + 1) * LANES)
    base = U_HIST - (CONV_K - 1)
    acc = ck_ref[0:1, cs] * ubuf[c, _every_other_row(r0 + base, rows), :]
    for k in range(1, CONV_K):
        acc = acc + ck_ref[k:k + 1, cs] * ubuf[c, _every_other_row(r0 + base + k, rows), :]
    return acc


def _mixer_out(x, pieces, first_row, cb_ref, lng_ref, lnb_ref, hm_ref, pw_ref, ps_ref, wout_ref,
               pbuf, *, full_windows):
    tl = x.shape[0]
    cv = jnp.concatenate([jnp.concatenate(pc, axis=0) for pc in pieces], axis=1) + cb_ref[...]

    hm = hm_ref[...]
    mean = jnp.concatenate(
        [_split_dot(cv[:, i:i + MXU_DIM], hm) for i in range(0, CONV_CH, MXU_DIM)], axis=1)
    d = cv - mean
    var = jnp.concatenate(
        [_split_dot(d[:, i:i + MXU_DIM] * d[:, i:i + MXU_DIM], hm)
         for i in range(0, CONV_CH, MXU_DIM)], axis=1)
    yc = (d * lax.rsqrt(var + EPS)) * lng_ref[...] + lnb_ref[...]
    yc = jax.nn.silu(yc)

    ys = []
    for gi, w in enumerate(POOL_WINDOWS):
        tok = pbuf[gi, _every_other_row(P_HIST, tl), :]
        s = tok
        for back in range(1, w):
            s = s + pbuf[gi, _every_other_row(P_HIST - back, tl), :]
        if full_windows:
            m = s * (1.0 / w)
        else:
            t = first_row + lax.broadcasted_iota(jnp.int32, (tl, 1), 0)
            m = s / jnp.minimum(t + 1, w).astype(F32)
        dg = (m - tok).astype(BF16)
        ys.append(jnp.dot(dg, pw_ref[gi], preferred_element_type=F32))
    yp = jnp.concatenate(ys, axis=-1) * ps_ref[...]

    out = x + jnp.dot(yc.astype(BF16), wout_ref[0:CONV_CH, :], preferred_element_type=F32)
    return out + jnp.dot(yp.astype(BF16), wout_ref[CONV_CH:, :], preferred_element_type=F32)


def _ffn_stage(x, g2_ref, wup_ref, fk_ref, wdn_ref, fg_ref, fhist, fbuf, after_chunk):
    tl = x.shape[0]
    hn = _rmsnorm(x, g2_ref[...]).astype(BF16)
    starts = [sum(FF_CHUNKS[:i]) for i in range(len(FF_CHUNKS))]

    def up_proj(ci):
        for half, base in enumerate((0, D_FF)):
            off, width = base + starts[ci], FF_CHUNKS[ci]
            up = jnp.dot(hn, wup_ref[:, off:off + width], preferred_element_type=F32)
            for s in range(width // LANES):
                cols = slice(off + s * LANES, off + (s + 1) * LANES)
                buf = fbuf.at[ci % 2, half, s]
                buf[_every_other_row(0, F_HIST), :] = fhist[:, cols]
                buf[_every_other_row(F_HIST, tl), :] = up[:, s * LANES:(s + 1) * LANES]
            fhist[:, off:off + width] = _tail(fhist[:, off:off + width], up, F_HIST)

    def gated(ci):
        conv = []
        for half, base in enumerate((0, D_FF)):
            off, width = base + starts[ci], FF_CHUNKS[ci]
            ys = []
            for s in range(width // LANES):
                cols = slice(off + s * LANES, off + (s + 1) * LANES)
                buf = fbuf.at[ci % 2, half, s]
                y = fk_ref[0:1, cols] * buf[_every_other_row(F_HIST - (FFN_CONV_K - 1), tl), :]
                for k in range(1, FFN_CONV_K):
                    y = y + fk_ref[k:k + 1, cols] * buf[
                        _every_other_row(F_HIST - (FFN_CONV_K - 1) + k, tl), :]
                ys.append(y)
            conv.append(jnp.concatenate(ys, axis=1))
        return (jax.nn.silu(conv[0]) * conv[1]).astype(BF16)

    acc = x
    up_proj(0)
    for ci, width in enumerate(FF_CHUNKS):
        if ci + 1 < len(FF_CHUNKS):
            up_proj(ci + 1)
        acc = acc + jnp.dot(gated(ci), wdn_ref[starts[ci]:starts[ci] + width, :],
                            preferred_element_type=F32)
        after_chunk(ci)
    if fg_ref is not None:
        acc = _rmsnorm(acc, fg_ref[...])
    return acc


def _layer_kernel(h_ref, g1_ref, win_ref, ck_ref, cb_ref, lng_ref, lnb_ref, hm_ref, pw_ref, ps_ref,
                  wout_ref, g2_ref, wup_ref, fk_ref, wdn_ref, cu_in_ref, cp_in_ref, cf_in_ref,
                  *rest, tl, nt, t0, final_norm, emit_carry):
    rest = list(rest)
    fg_ref = rest.pop(0) if final_norm else None
    out_ref = rest.pop(0)
    carry_out = [rest.pop(0) for _ in range(3)] if emit_carry else None
    uhist, phist, ubuf, pbuf, fhist, fbuf, hmid = rest
    g = pl.program_id(0)
    tile = lax.rem(g, nt)

    @pl.when(tile == 0)
    def _():
        uhist[...] = cu_in_ref[...]
        phist[...] = cp_in_ref[...]

    @pl.when(g == 0)
    def _():
        hmid[1] = jnp.zeros((tl, D_MODEL), F32)

    @pl.when(jnp.logical_or(g == 0, lax.rem(g + nt - 1, nt) == 0))
    def _():
        fhist[...] = cf_in_ref[...]

    xm = h_ref[0]
    _mixer_in(xm, g1_ref, win_ref, uhist, phist, ubuf, pbuf)

    rows = min(CONV_ROWS, tl)
    slabs = CONV_CH // LANES
    jobs = [(c, r0) for r0 in range(0, tl, rows) for c in range(slabs)]
    pieces = [[] for _ in range(slabs)]

    def conv_share(ci):
        left = len(FF_CHUNKS) - ci
        for _ in range(-(-len(jobs) // left)):
            c, r0 = jobs.pop(0)
            pieces[c].append(_conv_piece(ck_ref, ubuf, c, r0, rows))

    out_ref[0] = _ffn_stage(hmid[lax.rem(g + 1, 2)], g2_ref, wup_ref, fk_ref, wdn_ref, fg_ref,
                            fhist, fbuf, conv_share)

    hmid[lax.rem(g, 2)] = _mixer_out(
        xm, pieces, t0 + tile * tl, cb_ref, lng_ref, lnb_ref, hm_ref, pw_ref, ps_ref, wout_ref,
        pbuf, full_windows=t0 + 1 >= max(POOL_WINDOWS))

    if emit_carry:
        carry_out[0][...] = uhist[...]
        carry_out[1][...] = phist[...]
        carry_out[2][...] = fhist[...]


def _layer_spec(arr, layer):
    zeros = (0,) * (arr.ndim - 1)
    return pl.BlockSpec((None,) + arr.shape[1:], lambda g: (layer,) + zeros,
                        pipeline_mode=pl.Buffered(1))


def _whole_spec(arr):
    zeros = (0,) * arr.ndim
    return pl.BlockSpec(arr.shape, lambda g: zeros, pipeline_mode=pl.Buffered(1))


def _layer(h, params, layer, carry, *, t0, final_g, emit_carry):
    bsz, length, _ = h.shape
    tl = min(SEQ_TILE, length)
    nt = length // tl
    steps = bsz * nt
    final_norm = final_g is not None

    def in_tile(g):
        t = jnp.minimum(g, steps - 1)
        return (t // nt, t % nt, 0)

    def out_tile(g):
        t = jnp.maximum(g - 1, 0)
        return (t // nt, t % nt, 0)

    operands = [h] + list(params) + list(carry) + ([final_g] if final_norm else [])
    in_specs = ([pl.BlockSpec((1, tl, D_MODEL), in_tile)]
                + [_layer_spec(a, layer) for a in params]
                + [_whole_spec(a) for a in carry]
                + ([_whole_spec(final_g)] if final_norm else []))
    out_shape = [jax.ShapeDtypeStruct(h.shape, F32)]
    out_specs = [pl.BlockSpec((1, tl, D_MODEL), out_tile)]
    if emit_carry:
        out_shape += [jax.ShapeDtypeStruct(c.shape, F32) for c in carry]
        out_specs += [pl.BlockSpec(c.shape, lambda g: (0, 0)) for c in carry]
    ff_buf = pltpu.VMEM(
        (2, 2, max(FF_CHUNKS) // LANES, ROW_PITCH * (F_HIST + tl), LANES), F32)
    return pl.pallas_call(
        functools.partial(_layer_kernel, tl=tl, nt=nt, t0=t0, final_norm=final_norm,
                          emit_carry=emit_carry),
        grid=(steps + 1,),
        in_specs=in_specs,
        out_specs=out_specs,
        out_shape=out_shape,
        scratch_shapes=[
            pltpu.VMEM((U_HIST, CONV_CH), F32),
            pltpu.VMEM((P_HIST, POOL_CH), F32),
            pltpu.VMEM((CONV_CH // LANES, ROW_PITCH * (U_HIST + tl), LANES), F32),
            pltpu.VMEM((POOL_CH // LANES, ROW_PITCH * (P_HIST + tl), LANES), F32),
            pltpu.VMEM((F_HIST, 2 * D_FF), F32),
            ff_buf,
            pltpu.VMEM((2, tl, D_MODEL), F32)],
        compiler_params=pltpu.CompilerParams(
            dimension_semantics=("arbitrary",), vmem_limit_bytes=VMEM_LIMIT_BYTES),
        name="layer_meta" if emit_carry else "layer",
    )(*operands)


def _head_mean_matrix(depth):
    head = jnp.arange(MXU_DIM) // CONV_HEAD_DIM
    hm = jnp.where(head[:, None] == head[None, :], 1.0 / CONV_HEAD_DIM, 0.0).astype(BF16)
    return jnp.broadcast_to(hm, (depth, MXU_DIM, MXU_DIM))


def kernel(x, meta_tokens, norm1_g, w_in, conv_dw_k, conv_dw_b, conv_ln_g, conv_ln_b, pool_w,
           pool_scale, w_out, norm2_g, w_up, ffn_dw_k, w_down, final_g):
    assert x.ndim == 3 and x.shape[2] == D_MODEL and x.shape[1] % SEQ_TILE == 0
    assert meta_tokens.shape == (N_META, D_MODEL)
    depth = w_in.shape[0]
    rows = lambda v: v.reshape(depth, 1, -1).astype(F32)
    params = (rows(norm1_g), w_in.astype(BF16), conv_dw_k.astype(F32), rows(conv_dw_b),
              rows(conv_ln_g), rows(conv_ln_b), _head_mean_matrix(depth), pool_w.astype(BF16),
              rows(pool_scale), w_out.astype(BF16), rows(norm2_g), w_up.astype(BF16),
              ffn_dw_k.astype(F32), w_down.astype(BF16))
    zero_carry = (jnp.zeros((U_HIST, CONV_CH), F32), jnp.zeros((P_HIST, POOL_CH), F32),
                  jnp.zeros((F_HIST, 2 * D_FF), F32))

    hx = x.astype(F32)
    hmeta = meta_tokens.astype(F32)[None]
    for i in range(depth):
        fg = final_g.reshape(1, -1).astype(F32) if i == depth - 1 else None
        hmeta, *carry = _layer(hmeta, params, i, zero_carry, t0=0, final_g=None, emit_carry=True)
        (hx,) = _layer(hx, params, i, carry, t0=N_META, final_g=fg, emit_carry=False)
    return hx.astype(x.dtype)
```

```python
import functools

import jax
import jax.numpy as jnp
from jax import lax
from jax.experimental import pallas as pl
from jax.experimental.pallas import tpu as pltpu

D_MODEL = 1024
N_META = 16
CONV_CH = 512
CONV_HEADS = 8
CONV_HEAD_DIM = CONV_CH // CONV_HEADS
CONV_K = 31
POOL_CH = 512
POOL_WINDOWS = (2, 4, 8, 16)
POOL_GROUP_DIM = POOL_CH // len(POOL_WINDOWS)
IN_COLS = 2 * CONV_CH + POOL_CH
D_FF = 2816
FFN_CONV_K = 3
EPS = 1e-6

SUBLANES = 8
LANES = 128
MXU_DIM = 256

U_HIST = 32
P_HIST = 16
F_HIST = 8
ROW_PITCH = 2
CONV_ROWS = 64
FF_CHUNKS = (768, 768, 768, 512)

SEQ_TILE = 512
VMEM_LIMIT_BYTES = 60 * 1024 * 1024

F32 = jnp.float32
BF16 = jnp.bfloat16


def _rmsnorm(x, g):
    ms = jnp.mean(x * x, axis=-1, keepdims=True)
    return (x * lax.rsqrt(ms + EPS)) * g


def _split_dot(x, w):
    hi = x.astype(BF16)
    lo = (x - hi.astype(F32)).astype(BF16)
    return (jnp.dot(hi, w, preferred_element_type=F32)
            + jnp.dot(lo, w, preferred_element_type=F32))


def _tail(old, new, n):
    m = new.shape[0]
    if m >= n:
        return new[m - n:]
    return jnp.concatenate([old[m:], new], axis=0)


def _every_other_row(start, size):
    return pl.ds(ROW_PITCH * start, size, stride=ROW_PITCH)


def _mixer_in(x, g1_ref, win_ref, uhist, phist, ubuf, pbuf):
    tl = x.shape[0]
    hn = _rmsnorm(x, g1_ref[...])
    z = jnp.dot(hn.astype(BF16), win_ref[...], preferred_element_type=F32)
    u = z[:, :CONV_CH] * jax.nn.sigmoid(z[:, CONV_CH:2 * CONV_CH])
    p = z[:, 2 * CONV_CH:]

    for c in range(CONV_CH // LANES):
        cs = slice(c * LANES, (c + 1) * LANES)
        ubuf[c, _every_other_row(0, U_HIST), :] = uhist[:, cs]
        ubuf[c, _every_other_row(U_HIST, tl), :] = u[:, cs]
        pbuf[c, _every_other_row(0, P_HIST), :] = phist[:, cs]
        pbuf[c, _every_other_row(P_HIST, tl), :] = p[:, cs]
    uhist[...] = _tail(uhist[...], u, U_HIST)
    phist[...] = _tail(phist[...], p, P_HIST)


def _conv_piece(ck_ref, ubuf, c, r0, rows):
    cs = slice(c * LANES, (c + 1) * LANES)
    base = U_HIST - (CONV_K - 1)
    acc = ck_ref[0:1, cs] * ubuf[c, _every_other_row(r0 + base, rows), :]
    for k in range(1, CONV_K):
        acc = acc + ck_ref[k:k + 1, cs] * ubuf[c, _every_other_row(r0 + base + k, rows), :]
    return acc


def _mixer_out(x, pieces, first_row, cb_ref, lng_ref, lnb_ref, hm_ref, pw_ref, ps_ref, wout_ref,
               pbuf, *, full_windows):
    tl = x.shape[0]
    cv = jnp.concatenate([jnp.concatenate(pc, axis=0) for pc in pieces], axis=1) + cb_ref[...]

    hm = hm_ref[...]
    mean = jnp.concatenate(
        [_split_dot(cv[:, i:i + MXU_DIM], hm) for i in range(0, CONV_CH, MXU_DIM)], axis=1)
    d = cv - mean
    dd = (d * d).astype(BF16)
    var = jnp.concatenate(
        [jnp.dot(dd[:, i:i + MXU_DIM], hm, preferred_element_type=F32)
         for i in range(0, CONV_CH, MXU_DIM)], axis=1)
    yc = (d * lax.rsqrt(var + EPS)) * lng_ref[...] + lnb_ref[...]
    yc = jax.nn.silu(yc)

    ys = []
    for gi, w in enumerate(POOL_WINDOWS):
        tok = pbuf[gi, _every_other_row(P_HIST, tl), :]
        s = tok
        for back in range(1, w):
            s = s + pbuf[gi, _every_other_row(P_HIST - back, tl), :]
        if full_windows:
            m = s * (1.0 / w)
        else:
            t = first_row + lax.broadcasted_iota(jnp.int32, (tl, 1), 0)
            m = s / jnp.minimum(t + 1, w).astype(F32)
        dg = (m - tok).astype(BF16)
        ys.append(jnp.dot(dg, pw_ref[gi], preferred_element_type=F32))
    yp = jnp.concatenate(ys, axis=-1) * ps_ref[...]

    out = x + jnp.dot(yc.astype(BF16), wout_ref[0:CONV_CH, :], preferred_element_type=F32)
    return out + jnp.dot(yp.astype(BF16), wout_ref[CONV_CH:, :], preferred_element_type=F32)


def _ffn_stage(x, g2_ref, wup_ref, fk_ref, wdn_ref, fg_ref, fhist, fbuf, after_chunk):
    tl = x.shape[0]
    hn = _rmsnorm(x, g2_ref[...]).astype(BF16)
    starts = [sum(FF_CHUNKS[:i]) for i in range(len(FF_CHUNKS))]

    def up_proj(ci):
        for half, base in enumerate((0, D_FF)):
            off, width = base + starts[ci], FF_CHUNKS[ci]
            up = jnp.dot(hn, wup_ref[:, off:off + width], preferred_element_type=F32)
            for s in range(width // LANES):
                cols = slice(off + s * LANES, off + (s + 1) * LANES)
                buf = fbuf.at[ci % 2, half, s]
                buf[_every_other_row(0, F_HIST), :] = fhist[:, cols]
                buf[_every_other_row(F_HIST, tl), :] = up[:, s * LANES:(s + 1) * LANES]
            fhist[:, off:off + width] = _tail(fhist[:, off:off + width], up, F_HIST)

    def gated(ci):
        conv = []
        for half, base in enumerate((0, D_FF)):
            off, width = base + starts[ci], FF_CHUNKS[ci]
            ys = []
            for s in range(width // LANES):
                cols = slice(off + s * LANES, off + (s + 1) * LANES)
                buf = fbuf.at[ci % 2, half, s]
                y = fk_ref[0:1, cols] * buf[_every_other_row(F_HIST - (FFN_CONV_K - 1), tl), :]
                for k in range(1, FFN_CONV_K):
                    y = y + fk_ref[k:k + 1, cols] * buf[
                        _every_other_row(F_HIST - (FFN_CONV_K - 1) + k, tl), :]
                ys.append(y)
            conv.append(jnp.concatenate(ys, axis=1))
        return (jax.nn.silu(conv[0]) * conv[1]).astype(BF16)

    acc = x
    up_proj(0)
    for ci, width in enumerate(FF_CHUNKS):
        if ci + 1 < len(FF_CHUNKS):
            up_proj(ci + 1)
        acc = acc + jnp.dot(gated(ci), wdn_ref[starts[ci]:starts[ci] + width, :],
                            preferred_element_type=F32)
        after_chunk(ci)
    if fg_ref is not None:
        acc = _rmsnorm(acc, fg_ref[...])
    return acc


def _layer_kernel(h_ref, g1_ref, win_ref, ck_ref, cb_ref, lng_ref, lnb_ref, hm_ref, pw_ref, ps_ref,
                  wout_ref, g2_ref, wup_ref, fk_ref, wdn_ref, cu_in_ref, cp_in_ref, cf_in_ref,
                  *rest, tl, nt, t0, final_norm, emit_carry):
    rest = list(rest)
    fg_ref = rest.pop(0) if final_norm else None
    out_ref = rest.pop(0)
    carry_out = [rest.pop(0) for _ in range(3)] if emit_carry else None
    uhist, phist, ubuf, pbuf, fhist, fbuf, hmid = rest
    g = pl.program_id(0)
    tile = lax.rem(g, nt)

    @pl.when(tile == 0)
    def _():
        uhist[...] = cu_in_ref[...]
        phist[...] = cp_in_ref[...]

    @pl.when(g == 0)
    def _():
        hmid[1] = jnp.zeros((tl, D_MODEL), F32)

    @pl.when(jnp.logical_or(g == 0, lax.rem(g + nt - 1, nt) == 0))
    def _():
        fhist[...] = cf_in_ref[...]

    xm = h_ref[0]
    _mixer_in(xm, g1_ref, win_ref, uhist, phist, ubuf, pbuf)

    rows = min(CONV_ROWS, tl)
    slabs = CONV_CH // LANES
    jobs = [(c, r0) for r0 in range(0, tl, rows) for c in range(slabs)]
    pieces = [[] for _ in range(slabs)]

    def conv_share(ci):
        left = len(FF_CHUNKS) - ci
        for _ in range(-(-len(jobs) // left)):
            c, r0 = jobs.pop(0)
            pieces[c].append(_conv_piece(ck_ref, ubuf, c, r0, rows))

    out_ref[0] = _ffn_stage(hmid[lax.rem(g + 1, 2)], g2_ref, wup_ref, fk_ref, wdn_ref, fg_ref,
                            fhist, fbuf, conv_share)

    hmid[lax.rem(g, 2)] = _mixer_out(
        xm, pieces, t0 + tile * tl, cb_ref, lng_ref, lnb_ref, hm_ref, pw_ref, ps_ref, wout_ref,
        pbuf, full_windows=t0 + 1 >= max(POOL_WINDOWS))

    if emit_carry:
        carry_out[0][...] = uhist[...]
        carry_out[1][...] = phist[...]
        carry_out[2][...] = fhist[...]


def _layer_spec(arr, layer):
    zeros = (0,) * (arr.ndim - 1)
    return pl.BlockSpec((None,) + arr.shape[1:], lambda g: (layer,) + zeros,
                        pipeline_mode=pl.Buffered(1))


def _whole_spec(arr):
    zeros = (0,) * arr.ndim
    return pl.BlockSpec(arr.shape, lambda g: zeros, pipeline_mode=pl.Buffered(1))


def _layer(h, params, layer, carry, *, t0, final_g, emit_carry):
    bsz, length, _ = h.shape
    tl = min(SEQ_TILE, length)
    nt = length // tl
    steps = bsz * nt
    final_norm = final_g is not None

    def in_tile(g):
        t = jnp.minimum(g, steps - 1)
        return (t // nt, t % nt, 0)

    def out_tile(g):
        t = jnp.maximum(g - 1, 0)
        return (t // nt, t % nt, 0)

    operands = [h] + list(params) + list(carry) + ([final_g] if final_norm else [])
    in_specs = ([pl.BlockSpec((1, tl, D_MODEL), in_tile)]
                + [_layer_spec(a, layer) for a in params]
                + [_whole_spec(a) for a in carry]
                + ([_whole_spec(final_g)] if final_norm else []))
    out_shape = [jax.ShapeDtypeStruct(h.shape, F32)]
    out_specs = [pl.BlockSpec((1, tl, D_MODEL), out_tile)]
    if emit_carry:
        out_shape += [jax.ShapeDtypeStruct(c.shape, F32) for c in carry]
        out_specs += [pl.BlockSpec(c.shape, lambda g: (0, 0)) for c in carry]
    ff_buf = pltpu.VMEM(
        (2, 2, max(FF_CHUNKS) // LANES, ROW_PITCH * (F_HIST + tl), LANES), F32)
    return pl.pallas_call(
        functools.partial(_layer_kernel, tl=tl, nt=nt, t0=t0, final_norm=final_norm,
                          emit_carry=emit_carry),
        grid=(steps + 1,),
        in_specs=in_specs,
        out_specs=out_specs,
        out_shape=out_shape,
        scratch_shapes=[
            pltpu.VMEM((U_HIST, CONV_CH), F32),
            pltpu.VMEM((P_HIST, POOL_CH), F32),
            pltpu.VMEM((CONV_CH // LANES, ROW_PITCH * (U_HIST + tl), LANES), F32),
            pltpu.VMEM((POOL_CH // LANES, ROW_PITCH * (P_HIST + tl), LANES), F32),
            pltpu.VMEM((F_HIST, 2 * D_FF), F32),
            ff_buf,
            pltpu.VMEM((2, tl, D_MODEL), F32)],
        compiler_params=pltpu.CompilerParams(
            dimension_semantics=("arbitrary",), vmem_limit_bytes=VMEM_LIMIT_BYTES),
        name="layer_meta" if emit_carry else "layer",
    )(*operands)


def _head_mean_matrix(depth):
    head = jnp.arange(MXU_DIM) // CONV_HEAD_DIM
    hm = jnp.where(head[:, None] == head[None, :], 1.0 / CONV_HEAD_DIM, 0.0).astype(BF16)
    return jnp.broadcast_to(hm, (depth, MXU_DIM, MXU_DIM))


def kernel(x, meta_tokens, norm1_g, w_in, conv_dw_k, conv_dw_b, conv_ln_g, conv_ln_b, pool_w,
           pool_scale, w_out, norm2_g, w_up, ffn_dw_k, w_down, final_g):
    assert x.ndim == 3 and x.shape[2] == D_MODEL and x.shape[1] % SEQ_TILE == 0
    assert meta_tokens.shape == (N_META, D_MODEL)
    depth = w_in.shape[0]
    rows = lambda v: v.reshape(depth, 1, -1).astype(F32)
    params = (rows(norm1_g), w_in.astype(BF16), conv_dw_k.astype(F32), rows(conv_dw_b),
              rows(conv_ln_g), rows(conv_ln_b), _head_mean_matrix(depth), pool_w.astype(BF16),
              rows(pool_scale), w_out.astype(BF16), rows(norm2_g), w_up.astype(BF16),
              ffn_dw_k.astype(F32), w_down.astype(BF16))
    zero_carry = (jnp.zeros((U_HIST, CONV_CH), F32), jnp.zeros((P_HIST, POOL_CH), F32),
                  jnp.zeros((F_HIST, 2 * D_FF), F32))

    hx = x.astype(F32)
    hmeta = meta_tokens.astype(F32)[None]
    for i in range(depth):
        fg = final_g.reshape(1, -1).astype(F32) if i == depth - 1 else None
        hmeta, *carry = _layer(hmeta, params, i, zero_carry, t0=0, final_g=None, emit_carry=True)
        (hx,) = _layer(hx, params, i, carry, t0=N_META, final_g=fg, emit_carry=False)
    return hx.astype(x.dtype)
```
